```python
import math
import jax, jax.numpy as jnp
from jax import lax
import numpy as np


D_MODEL = 2048
BATCH = 2
SEQ = 8192
DEPTH = 4
DEC_BATCH = 16
DEC_SEQ = 2048
PAST_LEN = 128

D_MIX = D_MODEL
W_CONV = D_MIX // 4
W_DIFF = D_MIX // 4
W_SSD = D_MIX // 4
W_MLSTM = D_MIX - W_CONV - W_DIFF - W_SSD
CONV_K = 31
DIFF_HEADS = 4
DIFF_V = W_DIFF // DIFF_HEADS
DIFF_QK = DIFF_V // 2
ATTN_BLOCK = 128
SSD_HEADDIM = 64
SSD_HEADS = W_SSD // SSD_HEADDIM
SSD_GROUPS = 2
SSD_STATE = 128
SSD_CONV_K = 5
SSD_CONV_CH = W_SSD + 2 * SSD_GROUPS * SSD_STATE
CHUNK = 128
MLSTM_HEADS = 4
MLSTM_HD = W_MLSTM // MLSTM_HEADS
N_MEM = 256
XATTN_HEADS = 4
XATTN_HD = D_MODEL // XATTN_HEADS
D_FF = -(-8 * D_MODEL // (3 * 256)) * 256
IN_COLS = 2 * W_CONV + 3 * W_DIFF + W_SSD + SSD_CONV_CH + 2 * SSD_HEADS + 4 * W_MLSTM + 4 * MLSTM_HEADS
RMS_EPS = 1e-6
LN_EPS = 1e-5

kernel_name = 'hybrid_bidir_encoder_parallel_groups'


def _rmsnorm(x, g):
    xf = x.astype(jnp.float32)
    y = xf * lax.rsqrt(jnp.mean(xf * xf, axis=-1, keepdims=True) + RMS_EPS)
    return (y * g.astype(jnp.float32)).astype(x.dtype)


def _layernorm(x, g, b):
    xf = x.astype(jnp.float32)
    mu = jnp.mean(xf, axis=-1, keepdims=True)
    var = jnp.mean(jnp.square(xf - mu), axis=-1, keepdims=True)
    y = (xf - mu) * lax.rsqrt(var + LN_EPS) * g.astype(jnp.float32) + b.astype(jnp.float32)
    return y.astype(x.dtype)


def _dwconv(x, w, bias):
    pad = w.shape[0] // 2
    y = lax.conv_general_dilated(x, w[:, None, :].astype(x.dtype), window_strides=(1,),
                                 padding=[(pad, pad)], dimension_numbers=('NWC', 'WIO', 'NWC'),
                                 feature_group_count=x.shape[-1])
    return y + bias.astype(x.dtype)


def _split_cols(u):
    sizes = [W_CONV, W_CONV, W_DIFF, W_DIFF, W_DIFF, W_SSD, SSD_CONV_CH, 2 * SSD_HEADS,
             W_MLSTM, W_MLSTM, W_MLSTM, W_MLSTM, 4 * MLSTM_HEADS]
    offsets = np.cumsum(sizes)[:-1].tolist()
    return jnp.split(u, offsets, axis=-1)


def _to_chunks(t):
    b, L = t.shape[:2]
    return jnp.moveaxis(t.reshape((b, L // CHUNK, CHUNK) + t.shape[2:]), 1, 0)


def _from_chunks(t):
    t = jnp.moveaxis(t, 0, 1)
    return t.reshape((t.shape[0], t.shape[1] * t.shape[2]) + t.shape[3:])


def _flip(t):
    return jnp.flip(t, axis=1)


def _diff_attention(q, k, v, lam):
    b, L = q.shape[:2]
    nb = L // ATTN_BLOCK
    scale = DIFF_QK ** -0.5
    slopes = jnp.exp2(-8.0 / DIFF_HEADS * jnp.arange(1, DIFF_HEADS + 1, dtype=jnp.float32))
    kpos = jnp.arange(L)
    qb = jnp.moveaxis(q.reshape((b, nb, ATTN_BLOCK) + q.shape[2:]), 1, 0)

    def block(args):
        qi, i = args
        s = jnp.einsum('bqhcd,bkhcd->bhcqk', qi, k).astype(jnp.float32) * scale
        qpos = i * ATTN_BLOCK + jnp.arange(ATTN_BLOCK)
        dist = jnp.abs(qpos[:, None] - kpos[None, :]).astype(jnp.float32)
        p = jax.nn.softmax(s - slopes[:, None, None, None] * dist, axis=-1)
        a = (p[:, :, 0] - lam * p[:, :, 1]).astype(v.dtype)
        return jnp.einsum('bhqk,bkhd->bqhd', a, v)

    out = lax.map(block, (qb, jnp.arange(nb)))
    return _from_chunks(out)


def _ssd_scan(x, dt, A, Bm, Cm):
    b, L, H, P = x.shape
    G, N = Bm.shape[2], Bm.shape[3]
    R = H // G
    f32 = jnp.float32
    mask = jnp.tril(jnp.ones((CHUNK, CHUNK), dtype=bool))[None, :, :, None, None]
    a_gr = A.reshape(G, R)

    def step(S, inp):
        xx, dd, BB, CC = inp
        xx = xx.astype(f32)
        BB = BB.astype(f32)
        CC = CC.astype(f32)
        acs = jnp.cumsum(dd * a_gr, axis=1)
        seg = jnp.where(mask, acs[:, :, None] - acs[:, None, :], -jnp.inf)
        M = jnp.einsum('bqgn,bkgn->bqkg', CC, BB)[..., None] * jnp.exp(seg) * dd[:, None]
        y = jnp.einsum('bqkgr,bkgrp->bqgrp', M, xx)
        y = y + jnp.einsum('bqgn,bgrpn->bqgrp', CC, S) * jnp.exp(acs)[..., None]
        decay = jnp.exp(acs[:, -1:] - acs) * dd
        S = jnp.exp(acs[:, -1])[..., None, None] * S + jnp.einsum('bkgn,bkgr,bkgrp->bgrpn', BB, decay, xx)
        return S, y.reshape(y.shape[0], CHUNK, H, P)

    S0 = jnp.zeros((b, G, R, P, N), f32)
    xs = (_to_chunks(x.reshape(b, L, G, R, P)), _to_chunks(dt.reshape(b, L, G, R)),
          _to_chunks(Bm), _to_chunks(Cm))
    _, ys = lax.scan(step, S0, xs)
    return _from_chunks(ys).astype(x.dtype)


def _mlstm_scan(q, k, v, log_i, log_f):
    b, L, H, Dh = q.shape
    f32 = jnp.float32
    mask = jnp.tril(jnp.ones((CHUNK, CHUNK), dtype=bool))

    def step(carry, inp):
        C, n, m = carry
        qq, kk, vv, li, lf = inp
        qq = qq.astype(f32)
        kk = kk.astype(f32)
        vv = vv.astype(f32)
        bc = jnp.moveaxis(jnp.cumsum(lf, axis=1), 1, 2)
        li = jnp.moveaxis(li, 1, 2)
        dmat = jnp.where(mask, bc[..., :, None] - bc[..., None, :] + li[..., None, :], -jnp.inf)
        inter = bc + m[..., None]
        m_t = jnp.maximum(jnp.max(dmat, axis=-1), inter)
        w = jnp.exp(dmat - m_t[..., None])
        w_inter = jnp.exp(inter - m_t)
        sw = jnp.einsum('bqhd,bkhd->bhqk', qq, kk) * w
        num = jnp.einsum('bhqk,bkhd->bhqd', sw, vv) + w_inter[..., None] * jnp.einsum('bhvd,bqhd->bhqv', C, qq)
        den = jnp.sum(sw, axis=-1) + w_inter * jnp.einsum('bhd,bqhd->bhq', n, qq)
        h = num / jnp.maximum(jnp.abs(den), jnp.exp(-m_t))[..., None]
        btot = bc[..., -1]
        ls = btot[..., None] - bc + li
        m_new = jnp.maximum(btot + m, jnp.max(ls, axis=-1))
        ws = jnp.exp(ls - m_new[..., None])
        wc = jnp.exp(btot + m - m_new)
        C = wc[..., None, None] * C + jnp.einsum('bhs,bshv,bshd->bhvd', ws, vv, kk)
        n = wc[..., None] * n + jnp.einsum('bhs,bshd->bhd', ws, kk)
        return (C, n, m_new), jnp.moveaxis(h, 1, 2)

    init = (jnp.zeros((b, H, Dh, Dh), f32), jnp.zeros((b, H, Dh), f32), jnp.zeros((b, H), f32))
    xs = (_to_chunks(q), _to_chunks(k), _to_chunks(v),
          _to_chunks(log_i.astype(f32)), _to_chunks(log_f.astype(f32)))
    _, hs = lax.scan(step, init, xs)
    return _from_chunks(hs).astype(q.dtype)


def _mixer(h, l, p):
    f32 = jnp.float32
    b, L, _ = h.shape
    u = h @ p['w_in'][l]
    (cv, cg, dq, dk, dv, z, xbc, dt_raw, mq, mk, mv, mo, mg) = _split_cols(u)

    a = cv * jax.nn.sigmoid(cg)
    a = _dwconv(a, p['conv_dw_w'][l], p['conv_dw_b'][l])
    a = jax.nn.silu(_layernorm(a, p['conv_ln_g'][l], p['conv_ln_b'][l]))
    y_conv = a @ p['conv_pw'][l]

    lam_init = 0.8 - 0.6 * math.exp(-0.3 * l)
    lp = p['diff_lambda'][l].astype(f32)
    lam = jnp.exp(jnp.sum(lp[0] * lp[1])) - jnp.exp(jnp.sum(lp[2] * lp[3])) + lam_init
    o = _diff_attention(dq.reshape(b, L, DIFF_HEADS, 2, DIFF_QK), dk.reshape(b, L, DIFF_HEADS, 2, DIFF_QK),
                        dv.reshape(b, L, DIFF_HEADS, DIFF_V), lam)
    y_diff = (_rmsnorm(o, p['diff_subln'][l]) * (1.0 - lam_init)).reshape(b, L, W_DIFF)

    xbc = jax.nn.silu(_dwconv(xbc, p['ssd_conv_w'][l], p['ssd_conv_b'][l]))
    xs, bm, cm = jnp.split(xbc, [W_SSD, W_SSD + SSD_GROUPS * SSD_STATE], axis=-1)
    xs = xs.reshape(b, L, SSD_HEADS, SSD_HEADDIM)
    bm = bm.reshape(b, L, SSD_GROUPS, SSD_STATE)
    cm = cm.reshape(b, L, SSD_GROUPS, SSD_STATE)
    dt = jax.nn.softplus(dt_raw.astype(f32).reshape(b, L, 2, SSD_HEADS) + p['ssd_dt_bias'][l].astype(f32))
    A = -jnp.exp(p['ssd_a_log'][l].astype(f32))
    y_fwd = _ssd_scan(xs, dt[:, :, 0], A[0], bm, cm)
    y_bwd = _flip(_ssd_scan(_flip(xs), _flip(dt[:, :, 1]), A[1], _flip(bm), _flip(cm)))
    y_s = y_fwd + y_bwd + p['ssd_d'][l][:, None].astype(xs.dtype) * xs
    y_ssd = _rmsnorm(y_s.reshape(b, L, W_SSD) * jax.nn.silu(z), p['ssd_norm'][l])

    g = mg.astype(f32).reshape(b, L, 2, 2, MLSTM_HEADS) + p['mlstm_gate_bias'][l].astype(f32)
    mq = mq.reshape(b, L, MLSTM_HEADS, MLSTM_HD)
    mk = mk.reshape(b, L, MLSTM_HEADS, MLSTM_HD) * (MLSTM_HD ** -0.5)
    mv = mv.reshape(b, L, MLSTM_HEADS, MLSTM_HD)
    h_fwd = _mlstm_scan(mq, mk, mv, g[:, :, 0, 0], jax.nn.log_sigmoid(g[:, :, 0, 1]))
    h_bwd = _flip(_mlstm_scan(_flip(mq), _flip(mk), _flip(mv), _flip(g[:, :, 1, 0]),
                              _flip(jax.nn.log_sigmoid(g[:, :, 1, 1]))))
    hm = jax.nn.sigmoid(mo) * (h_fwd + h_bwd).reshape(b, L, W_MLSTM)
    y_mlstm = _rmsnorm(hm.reshape(b, L, MLSTM_HEADS, MLSTM_HD),
                       p['mlstm_norm'][l].reshape(MLSTM_HEADS, MLSTM_HD)).reshape(b, L, W_MLSTM)

    y = jnp.concatenate([y_conv, y_diff, y_ssd, y_mlstm], axis=-1)
    return y @ p['w_out'][l]


def _cross_attention(x, mem, wq, wk, wv, wo):
    b, L, _ = x.shape
    M = mem.shape[1]
    q = (x @ wq).reshape(b, L, XATTN_HEADS, XATTN_HD)
    k = (mem @ wk).reshape(b, M, XATTN_HEADS, XATTN_HD)
    v = (mem @ wv).reshape(b, M, XATTN_HEADS, XATTN_HD)
    s = jnp.einsum('bqhd,bkhd->bhqk', q, k).astype(jnp.float32) * (XATTN_HD ** -0.5)
    pr = jax.nn.softmax(s, axis=-1).astype(v.dtype)
    o = jnp.einsum('bhqk,bkhd->bqhd', pr, v).reshape(b, L, D_MODEL)
    return o @ wo


def _swiglu(x, wg, wu, wd):
    return (jax.nn.silu(x @ wg) * (x @ wu)) @ wd


def _trunk(x, mem, p):
    for l in range(DEPTH):
        x = x + _mixer(_rmsnorm(x, p['mix_norm'][l]), l, p)
        x = x + _cross_attention(_rmsnorm(x, p['xattn_norm'][l]), _rmsnorm(mem, p['mem_norm'][l]),
                                 p['w_xq'][l], p['w_xk'][l], p['w_xv'][l], p['w_xo'][l])
        x = x + _swiglu(_rmsnorm(x, p['ffn_norm'][l]), p['w_gate'][l], p['w_up'][l], p['w_down'][l])
    return _rmsnorm(x, p['final_norm'])


def setup_inputs(seed: int = 0) -> dict:
    key = jax.random.key(seed)
    ks = iter(jax.random.split(key, 48))
    f32 = jnp.float32

    def nrm(shape, scale):
        return scale * jax.random.normal(next(ks), shape, f32)

    def gain(shape):
        return 1.0 + nrm(shape, 0.02)

    x_prompt = nrm((BATCH, SEQ, D_MODEL), 1.0)
    x_sample = nrm((DEC_BATCH, DEC_SEQ, D_MODEL), 1.0)
    mem_prompt = nrm((BATCH, N_MEM, D_MODEL), 1.0)
    mem_sample = nrm((DEC_BATCH, N_MEM, D_MODEL), 1.0)
    mix_norm = gain((DEPTH, D_MODEL))
    w_in = nrm((DEPTH, D_MODEL, IN_COLS), D_MODEL ** -0.5)
    conv_dw_w = nrm((DEPTH, CONV_K, W_CONV), CONV_K ** -0.5)
    conv_dw_b = nrm((DEPTH, W_CONV), 0.02)
    conv_ln_g = gain((DEPTH, W_CONV))
    conv_ln_b = nrm((DEPTH, W_CONV), 0.02)
    conv_pw = nrm((DEPTH, W_CONV, W_CONV), W_CONV ** -0.5)
    diff_lambda = nrm((DEPTH, 4, DIFF_QK), 0.1)
    diff_subln = gain((DEPTH, DIFF_V))
    ssd_conv_w = nrm((DEPTH, SSD_CONV_K, SSD_CONV_CH), SSD_CONV_K ** -0.5)
    ssd_conv_b = nrm((DEPTH, SSD_CONV_CH), 0.02)
    ssd_a_log = jnp.log(jax.random.uniform(next(ks), (DEPTH, 2, SSD_HEADS), f32, 1.0, 16.0))
    dt0 = jnp.exp(jax.random.uniform(next(ks), (DEPTH, 2, SSD_HEADS), f32, math.log(1e-3), math.log(1e-1)))
    ssd_dt_bias = dt0 + jnp.log(-jnp.expm1(-dt0))
    ssd_d = gain((DEPTH, SSD_HEADS))
    ssd_norm = gain((DEPTH, W_SSD))
    mlstm_gate_bias = jnp.concatenate(
        [nrm((DEPTH, 2, 1, MLSTM_HEADS), 0.1),
         jnp.linspace(3.0, 6.0, MLSTM_HEADS, dtype=f32) + nrm((DEPTH, 2, 1, MLSTM_HEADS), 0.1)], axis=2)
    mlstm_norm = gain((DEPTH, W_MLSTM))
    w_out = nrm((DEPTH, D_MIX, D_MODEL), D_MIX ** -0.5)
    xattn_norm = gain((DEPTH, D_MODEL))
    mem_norm = gain((DEPTH, D_MODEL))
    w_xq = nrm((DEPTH, D_MODEL, D_MODEL), D_MODEL ** -0.5)
    w_xk = nrm((DEPTH, D_MODEL, D_MODEL), D_MODEL ** -0.5)
    w_xv = nrm((DEPTH, D_MODEL, D_MODEL), D_MODEL ** -0.5)
    w_xo = nrm((DEPTH, D_MODEL, D_MODEL), D_MODEL ** -0.5)
    ffn_norm = gain((DEPTH, D_MODEL))
    w_gate = nrm((DEPTH, D_MODEL, D_FF), D_MODEL ** -0.5)
    w_up = nrm((DEPTH, D_MODEL, D_FF), D_MODEL ** -0.5)
    w_down = nrm((DEPTH, D_FF, D_MODEL), D_FF ** -0.5)
    final_norm = gain((D_MODEL,))
    return {'x_prompt': x_prompt, 'x_sample': x_sample, 'mem_prompt': mem_prompt, 'mem_sample': mem_sample,
            'mix_norm': mix_norm, 'w_in': w_in, 'conv_dw_w': conv_dw_w, 'conv_dw_b': conv_dw_b,
            'conv_ln_g': conv_ln_g, 'conv_ln_b': conv_ln_b, 'conv_pw': conv_pw,
            'diff_lambda': diff_lambda, 'diff_subln': diff_subln,
            'ssd_conv_w': ssd_conv_w, 'ssd_conv_b': ssd_conv_b, 'ssd_a_log': ssd_a_log,
            'ssd_dt_bias': ssd_dt_bias, 'ssd_d': ssd_d, 'ssd_norm': ssd_norm,
            'mlstm_gate_bias': mlstm_gate_bias, 'mlstm_norm': mlstm_norm, 'w_out': w_out,
            'xattn_norm': xattn_norm, 'mem_norm': mem_norm, 'w_xq': w_xq, 'w_xk': w_xk, 'w_xv': w_xv,
            'w_xo': w_xo, 'ffn_norm': ffn_norm, 'w_gate': w_gate, 'w_up': w_up, 'w_down': w_down,
            'final_norm': final_norm}


def reference(x_prompt, x_sample, mem_prompt, mem_sample, mix_norm, w_in, conv_dw_w, conv_dw_b,
              conv_ln_g, conv_ln_b, conv_pw, diff_lambda, diff_subln, ssd_conv_w, ssd_conv_b,
              ssd_a_log, ssd_dt_bias, ssd_d, ssd_norm, mlstm_gate_bias, mlstm_norm, w_out,
              xattn_norm, mem_norm, w_xq, w_xk, w_xv, w_xo, ffn_norm, w_gate, w_up, w_down,
              final_norm):
    p = dict(mix_norm=mix_norm, w_in=w_in, conv_dw_w=conv_dw_w, conv_dw_b=conv_dw_b,
             conv_ln_g=conv_ln_g, conv_ln_b=conv_ln_b, conv_pw=conv_pw,
             diff_lambda=diff_lambda, diff_subln=diff_subln,
             ssd_conv_w=ssd_conv_w, ssd_conv_b=ssd_conv_b, ssd_a_log=ssd_a_log,
             ssd_dt_bias=ssd_dt_bias, ssd_d=ssd_d, ssd_norm=ssd_norm,
             mlstm_gate_bias=mlstm_gate_bias, mlstm_norm=mlstm_norm, w_out=w_out,
             xattn_norm=xattn_norm, mem_norm=mem_norm, w_xq=w_xq, w_xk=w_xk, w_xv=w_xv, w_xo=w_xo,
             ffn_norm=ffn_norm, w_gate=w_gate, w_up=w_up, w_down=w_down, final_norm=final_norm)
    y_prompt = _trunk(x_prompt, mem_prompt, p)
    y_sample = _trunk(x_sample, mem_sample, p)
    return (y_prompt, y_sample)
```

```python
import functools
import math

import jax
import jax.numpy as jnp
from jax import lax
from jax.experimental import pallas as pl
from jax.experimental.pallas import tpu as pltpu

F32 = jnp.float32
BF16 = jnp.bfloat16
HIGHEST = lax.Precision.HIGHEST

D_MODEL = 2048
DEPTH = 4
W_CONV = 512
W_DIFF = 512
W_SSD = 512
W_MLSTM = 512
CONV_K = 31
DIFF_HEADS = 4
DIFF_V = 128
DIFF_QK = 64
SSD_HEADDIM = 64
SSD_HEADS = 8
SSD_GROUPS = 2
SSD_STATE = 128
SSD_CONV_K = 5
SSD_CONV_CH = W_SSD + 2 * SSD_GROUPS * SSD_STATE
CHUNK = 128
MLSTM_HEADS = 4
MLSTM_HD = 128
XATTN_HEADS = 4
XATTN_HD = D_MODEL // XATTN_HEADS
D_FF = 5632
RMS_EPS = 1e-6
LN_EPS = 1e-5

LANES = 128
HALO = 16
U_COLS = 6144
G_COLS = 256
MLSTM_GATE_LANE = 16
VMEM_LIMIT = 52 * 1024 * 1024


def _cparams(*sem):
    return pltpu.CompilerParams(dimension_semantics=sem, vmem_limit_bytes=VMEM_LIMIT)


def _sigmoid(x):
    return 1.0 / (1.0 + jnp.exp(-x))


def _silu(x):
    return x * _sigmoid(x)


def _softplus(x):
    return jnp.maximum(x, 0.0) + jnp.log(1.0 + jnp.exp(-jnp.abs(x)))


def _dot(a, b):
    return jnp.dot(a, b, preferred_element_type=F32)


def _dot_nt(a, b):
    return lax.dot_general(a, b, (((1,), (1,)), ((), ())), preferred_element_type=F32)


def _dot_tn(a, b):
    return lax.dot_general(a, b, (((0,), (0,)), ((), ())), preferred_element_type=F32)


def _dot_f32(a, b):
    return jnp.dot(a, b, preferred_element_type=F32, precision=HIGHEST)


def _rms_rows(x, gain):
    ms = jnp.mean(x * x, axis=-1, keepdims=True)
    return x * lax.rsqrt(ms + RMS_EPS) * gain


def _norm_proj_kernel(x_ref, gain_ref, w_ref, o_ref, xn_ref):
    @pl.when(pl.program_id(1) == 0)
    def _():
        xn_ref[...] = _rms_rows(x_ref[...], gain_ref[...]).astype(BF16)

    o_ref[...] = _dot(xn_ref[...], w_ref[...]).astype(o_ref.dtype)


def _norm_proj_gates_kernel(x_ref, gain_ref, w_ref, wg_ref, o_ref, og_ref, xn_ref):
    @pl.when(pl.program_id(1) == 0)
    def _():
        xn_ref[...] = _rms_rows(x_ref[...], gain_ref[...]).astype(BF16)
        og_ref[...] = _dot(xn_ref[...], wg_ref[...])

    o_ref[...] = _dot(xn_ref[...], w_ref[...]).astype(o_ref.dtype)


def _row_tile(t):
    return 1024 if t % 1024 == 0 else 512


def norm_proj(x, gain, w, wg=None, tn=512):
    t, d = x.shape
    n = w.shape[1]
    tm = _row_tile(t)
    grid = (t // tm, n // tn)
    x_spec = pl.BlockSpec((tm, d), lambda i, j: (i, 0))
    gain_spec = pl.BlockSpec((1, d), lambda i, j: (0, 0))
    w_spec = pl.BlockSpec((d, tn), lambda i, j: (0, j))
    o_spec = pl.BlockSpec((tm, tn), lambda i, j: (i, j))
    scratch = [pltpu.VMEM((tm, d), BF16)]
    if wg is None:
        return pl.pallas_call(
            _norm_proj_kernel, grid=grid,
            in_specs=[x_spec, gain_spec, w_spec], out_specs=o_spec,
            out_shape=jax.ShapeDtypeStruct((t, n), BF16), scratch_shapes=scratch,
            compiler_params=_cparams("parallel", "arbitrary"), name="norm_proj",
        )(x, gain, w)
    ng = wg.shape[1]
    return pl.pallas_call(
        _norm_proj_gates_kernel, grid=grid,
        in_specs=[x_spec, gain_spec, w_spec, pl.BlockSpec((d, ng), lambda i, j: (0, 0))],
        out_specs=[o_spec, pl.BlockSpec((tm, ng), lambda i, j: (i, 0))],
        out_shape=[jax.ShapeDtypeStruct((t, n), BF16), jax.ShapeDtypeStruct((t, ng), F32)],
        scratch_shapes=scratch,
        compiler_params=_cparams("parallel", "arbitrary"), name="norm_proj_gates",
    )(x, gain, w, wg)


def _norm_swiglu_kernel(x_ref, gain_ref, wg_ref, wu_ref, o_ref, xn_ref):
    @pl.when(pl.program_id(1) == 0)
    def _():
        xn_ref[...] = _rms_rows(x_ref[...], gain_ref[...]).astype(BF16)

    xn = xn_ref[...]
    o_ref[...] = (_silu(_dot(xn, wg_ref[...])) * _dot(xn, wu_ref[...])).astype(o_ref.dtype)


def norm_swiglu(x, gain, wg, wu, tn=512):
    t, d = x.shape
    n = wg.shape[1]
    tm = _row_tile(t)
    w_spec = pl.BlockSpec((d, tn), lambda i, j: (0, j))
    return pl.pallas_call(
        _norm_swiglu_kernel, grid=(t // tm, n // tn),
        in_specs=[pl.BlockSpec((tm, d), lambda i, j: (i, 0)),
                  pl.BlockSpec((1, d), lambda i, j: (0, 0)), w_spec, w_spec],
        out_specs=pl.BlockSpec((tm, tn), lambda i, j: (i, j)),
        out_shape=jax.ShapeDtypeStruct((t, n), BF16),
        scratch_shapes=[pltpu.VMEM((tm, d), BF16)],
        compiler_params=_cparams("parallel", "arbitrary"), name="norm_swiglu",
    )(x, gain, wg, wu)


def _mm_residual_kernel(*refs, n_in):
    res_ref = refs[0]
    a_refs = refs[1:1 + n_in]
    w_refs = refs[1 + n_in:1 + 2 * n_in]
    o_ref = refs[1 + 2 * n_in]
    acc = res_ref[...]
    for a_ref, w_ref in zip(a_refs, w_refs):
        acc = acc + _dot(a_ref[...], w_ref[...])
    o_ref[...] = acc


def mm_residual(res, acts, w, tn=512):
    t, n = res.shape
    k = acts[0].shape[1]
    n_in = len(acts)
    tm = 512
    a_spec = pl.BlockSpec((tm, k), lambda i, j: (i, 0))
    w_specs = [pl.BlockSpec((k, tn), functools.partial(lambda i, j, g: (g, j), g=g)) for g in range(n_in)]
    rspec = pl.BlockSpec((tm, tn), lambda i, j: (i, j))
    return pl.pallas_call(
        functools.partial(_mm_residual_kernel, n_in=n_in), grid=(t // tm, n // tn),
        in_specs=[rspec] + [a_spec] * n_in + w_specs, out_specs=rspec,
        out_shape=jax.ShapeDtypeStruct((t, n), F32),
        compiler_params=_cparams("parallel", "arbitrary"), name="mm_residual",
    )(res, *acts, *([w] * n_in))


def _rmsnorm_kernel(x_ref, gain_ref, o_ref):
    o_ref[...] = _rms_rows(x_ref[...], gain_ref[...])


def rmsnorm(x, gain):
    t, d = x.shape
    tm = 512
    return pl.pallas_call(
        _rmsnorm_kernel, grid=(t // tm,),
        in_specs=[pl.BlockSpec((tm, d), lambda i: (i, 0)), pl.BlockSpec((1, d), lambda i: (0, 0))],
        out_specs=pl.BlockSpec((tm, d), lambda i: (i, 0)),
        out_shape=jax.ShapeDtypeStruct((t, d), F32),
        compiler_params=_cparams("parallel"), name="final_rmsnorm",
    )(x, gain)


CONV_ROWS = 32


def _fill_ext(ext_ref, prev, cur, nxt, tl):
    i = pl.program_id(1)
    last = pl.num_programs(1) - 1
    ext_ref[pl.ds(0, HALO), :] = jnp.where(i > 0, prev, 0.0)
    ext_ref[pl.ds(HALO, tl), :] = cur
    ext_ref[pl.ds(HALO + tl, HALO), :] = jnp.where(i < last, nxt, 0.0)


def _dwconv_rows(ext_ref, w_ref, bias_ref, n_taps, r0):
    pad = n_taps // 2
    acc = None
    for k in range(n_taps):
        term = w_ref[pl.ds(k, 1), :] * ext_ref[pl.ds(HALO + r0 + k - pad, CONV_ROWS), :]
        acc = term if acc is None else acc + term
    return acc + bias_ref[...]


def _conv_branch_kernel(cvp_ref, cgp_ref, cv_ref, cg_ref, cvn_ref, cgn_ref,
                        w_ref, b_ref, lng_ref, lnb_ref, pw_ref, o_ref, ext_ref, act_ref, *, tl):
    def glu(v_ref, g_ref):
        return v_ref[...].astype(F32) * _sigmoid(g_ref[...].astype(F32))

    _fill_ext(ext_ref, glu(cvp_ref, cgp_ref), glu(cv_ref, cg_ref), glu(cvn_ref, cgn_ref), tl)
    for r0 in range(0, tl, CONV_ROWS):
        y = _dwconv_rows(ext_ref, w_ref, b_ref, CONV_K, r0)
        mu = jnp.mean(y, axis=-1, keepdims=True)
        yc = y - mu
        var = jnp.mean(yc * yc, axis=-1, keepdims=True)
        yn = yc * lax.rsqrt(var + LN_EPS) * lng_ref[...] + lnb_ref[...]
        act_ref[pl.ds(r0, CONV_ROWS), :] = _silu(yn).astype(BF16)
    o_ref[...] = _dot(act_ref[...], pw_ref[...]).astype(o_ref.dtype)


def _halo_specs(seq_len, tl, width, col):
    r = tl // HALO
    last_halo = seq_len // HALO - 1

    def prev_map(b, i):
        return (b, jnp.maximum(i * r - 1, 0), col)

    def cur_map(b, i):
        return (b, i, col)

    def next_map(b, i):
        return (b, jnp.minimum((i + 1) * r, last_halo), col)

    return (pl.BlockSpec((None, HALO, width), prev_map),
            pl.BlockSpec((None, tl, width), cur_map),
            pl.BlockSpec((None, HALO, width), next_map))


def conv_branch(u, w, bias, ln_g, ln_b, pw, tl=256):
    b, L, _ = u.shape
    vp, vc, vn = _halo_specs(L, tl, W_CONV, 0)
    gp, gc, gn = _halo_specs(L, tl, W_CONV, 1)
    row = pl.BlockSpec((1, W_CONV), lambda b, i: (0, 0))
    return pl.pallas_call(
        functools.partial(_conv_branch_kernel, tl=tl), grid=(b, L // tl),
        in_specs=[vp, gp, vc, gc, vn, gn,
                  pl.BlockSpec((CONV_K, W_CONV), lambda b, i: (0, 0)), row, row, row,
                  pl.BlockSpec((W_CONV, W_CONV), lambda b, i: (0, 0))],
        out_specs=pl.BlockSpec((None, tl, W_CONV), lambda b, i: (b, i, 0)),
        out_shape=jax.ShapeDtypeStruct((b, L, W_CONV), BF16),
        scratch_shapes=[pltpu.VMEM((tl + 2 * HALO, W_CONV), F32), pltpu.VMEM((tl, W_CONV), BF16)],
        compiler_params=_cparams("parallel", "arbitrary"), name="conv_branch",
    )(u, u, u, u, u, u, w, bias, ln_g, ln_b, pw)


def _ssd_conv_kernel(xp_ref, x_ref, xn_ref, w_ref, b_ref, o_ref, ext_ref, *, tl):
    _fill_ext(ext_ref, xp_ref[...].astype(F32), x_ref[...].astype(F32), xn_ref[...].astype(F32), tl)
    for r0 in range(0, tl, CONV_ROWS):
        y = _dwconv_rows(ext_ref, w_ref, b_ref, SSD_CONV_K, r0)
        o_ref[pl.ds(r0, CONV_ROWS), :] = _silu(y).astype(o_ref.dtype)


def ssd_conv(u, w, bias, tl=256):
    b, L, _ = u.shape
    xp, xc, xn = _halo_specs(L, tl, SSD_CONV_CH, 3)
    return pl.pallas_call(
        functools.partial(_ssd_conv_kernel, tl=tl), grid=(b, L // tl),
        in_specs=[xp, xc, xn, pl.BlockSpec((SSD_CONV_K, SSD_CONV_CH), lambda b, i: (0, 0)),
                  pl.BlockSpec((1, SSD_CONV_CH), lambda b, i: (0, 0))],
        out_specs=pl.BlockSpec((None, tl, SSD_CONV_CH), lambda b, i: (b, i, 0)),
        out_shape=jax.ShapeDtypeStruct((b, L, SSD_CONV_CH), BF16),
        scratch_shapes=[pltpu.VMEM((tl + 2 * HALO, SSD_CONV_CH), F32)],
        compiler_params=_cparams("parallel", "arbitrary"), name="ssd_conv",
    )(u, u, u, w, bias)


def _diff_attn_kernel(sc_ref, q_ref, k_ref, v_ref, gain_ref, o_ref,
                      m0_ref, l0_ref, a0_ref, m1_ref, l1_ref, a1_ref, *, bq, bk, out_scale):
    h = pl.program_id(1)
    qi = pl.program_id(2)
    kj = pl.program_id(3)

    @pl.when(kj == 0)
    def _():
        for m_ref, l_ref, a_ref in ((m0_ref, l0_ref, a0_ref), (m1_ref, l1_ref, a1_ref)):
            m_ref[...] = jnp.full(m_ref.shape, -jnp.inf, F32)
            l_ref[...] = jnp.zeros(l_ref.shape, F32)
            a_ref[...] = jnp.zeros(a_ref.shape, F32)

    q = q_ref[...]
    k = k_ref[...]
    v = v_ref[...]
    lane = lax.broadcasted_iota(jnp.int32, q.shape, 1)
    zero = jnp.zeros_like(q)
    qpos = qi * bq + lax.broadcasted_iota(jnp.int32, (bq, bk), 0)
    kpos = kj * bk + lax.broadcasted_iota(jnp.int32, (bq, bk), 1)
    bias = sc_ref[h] * jnp.abs(qpos - kpos).astype(F32)
    scale = DIFF_QK ** -0.5
    for first, m_ref, l_ref, a_ref in ((True, m0_ref, l0_ref, a0_ref), (False, m1_ref, l1_ref, a1_ref)):
        qh = jnp.where(lane < DIFF_QK, q, zero) if first else jnp.where(lane >= DIFF_QK, q, zero)
        s = _dot_nt(qh, k) * scale - bias
        m_old = m_ref[...]
        m_new = jnp.maximum(m_old, jnp.max(s, axis=-1, keepdims=True))
        alpha = jnp.exp(m_old - m_new)
        p = jnp.exp(s - m_new)
        l_ref[...] = alpha * l_ref[...] + jnp.sum(p, axis=-1, keepdims=True)
        a_ref[...] = alpha * a_ref[...] + _dot(p.astype(BF16), v)
        m_ref[...] = m_new

    @pl.when(kj == pl.num_programs(3) - 1)
    def _():
        lam = sc_ref[DIFF_HEADS]
        o = a0_ref[...] / l0_ref[...] - lam * (a1_ref[...] / l1_ref[...])
        o_ref[...] = (_rms_rows(o, gain_ref[...]) * out_scale).astype(o_ref.dtype)


def diff_attention(u, scalars, subln, out_scale, bq=512, bk=512):
    b, L, _ = u.shape
    q_blk, k_blk, v_blk = 1024 // LANES, 1536 // LANES, 2048 // LANES
    small = lambda: pltpu.VMEM((bq, 1), F32)
    big = lambda: pltpu.VMEM((bq, DIFF_V), F32)
    return pl.pallas_call(
        functools.partial(_diff_attn_kernel, bq=bq, bk=bk, out_scale=out_scale),
        grid=(b, DIFF_HEADS, L // bq, L // bk),
        in_specs=[pl.BlockSpec(memory_space=pltpu.SMEM),
                  pl.BlockSpec((None, bq, LANES), lambda b, h, i, j: (b, i, q_blk + h)),
                  pl.BlockSpec((None, bk, LANES), lambda b, h, i, j: (b, j, k_blk + h)),
                  pl.BlockSpec((None, bk, LANES), lambda b, h, i, j: (b, j, v_blk + h)),
                  pl.BlockSpec((1, DIFF_V), lambda b, h, i, j: (0, 0))],
        out_specs=pl.BlockSpec((None, bq, DIFF_V), lambda b, h, i, j: (b, i, h)),
        out_shape=jax.ShapeDtypeStruct((b, L, W_DIFF), BF16),
        scratch_shapes=[small(), small(), big(), small(), small(), big()],
        compiler_params=_cparams("parallel", "parallel", "parallel", "arbitrary"), name="diff_attention",
    )(scalars, u, u, u, subln)


def _tri(direction):
    r = lax.broadcasted_iota(jnp.int32, (CHUNK, CHUNK), 0)
    c = lax.broadcasted_iota(jnp.int32, (CHUNK, CHUNK), 1)
    keep = (c <= r) if direction == 0 else (c >= r)
    return keep, jnp.where(keep, 1.0, 0.0).astype(F32)


def _ssd_scan_kernel(xf_ref, gf_ref, xb_ref, gb_ref, pc_ref, yf_ref, yb_ref, st_ref):
    @pl.when(pl.program_id(1) == 0)
    def _():
        st_ref[...] = jnp.zeros(st_ref.shape, F32)

    hp = SSD_HEADS // SSD_GROUPS * SSD_HEADDIM
    lane_x = lax.broadcasted_iota(jnp.int32, (CHUNK, LANES), 1)
    for d, (x_ref, g_ref, y_ref) in enumerate(((xf_ref, gf_ref, yf_ref), (xb_ref, gb_ref, yb_ref))):
        keep, tri = _tri(d)
        edge = CHUNK - 1 if d == 0 else 0
        dt = _softplus(g_ref[...] + pc_ref[0:1, :])
        cum = _dot_f32(tri, dt * (-jnp.exp(pc_ref[2:3, :])))
        dt_t = dt.T
        cum_t = cum.T
        er = lax.broadcasted_iota(jnp.int32, (LANES, W_SSD), 0)
        ec = lax.broadcasted_iota(jnp.int32, (LANES, W_SSD), 1)
        expand = jnp.where(er == ec // SSD_HEADDIM + d * SSD_HEADS, 1.0, 0.0).astype(F32)
        ecum = _dot_f32(jnp.exp(cum), expand)
        decay = _dot_f32(jnp.exp(cum[edge:edge + 1, :] - cum) * dt, expand)

        xs = x_ref[:, 0:W_SSD]
        bmat = x_ref[:, W_SSD:W_SSD + SSD_GROUPS * SSD_STATE]
        cmat = x_ref[:, W_SSD + SSD_GROUPS * SSD_STATE:]
        xd = (xs.astype(F32) * decay).astype(BF16)
        y_parts = []
        for g in range(SSD_GROUPS):
            b_g = bmat[:, g * SSD_STATE:(g + 1) * SSD_STATE]
            c_g = cmat[:, g * SSD_STATE:(g + 1) * SSD_STATE]
            cb = _dot_nt(c_g, b_g)
            st = st_ref[d, g]
            y_inter = _dot(c_g, st.astype(BF16)) * ecum[:, g * hp:(g + 1) * hp]
            pair_out = []
            for pr in range(SSD_HEADS // SSD_GROUPS // 2):
                ms = []
                for hh in range(2):
                    col = d * SSD_HEADS + g * (SSD_HEADS // SSD_GROUPS) + pr * 2 + hh
                    seg = cum[:, col:col + 1] - cum_t[col:col + 1, :]
                    decay_qk = jnp.exp(jnp.where(keep, seg, -jnp.inf))
                    ms.append((cb * decay_qk * dt_t[col:col + 1, :]).astype(BF16))
                x_pair = xs[:, g * hp + pr * LANES:g * hp + (pr + 1) * LANES]
                zero = jnp.zeros_like(x_pair)
                x_lo = jnp.where(lane_x < SSD_HEADDIM, x_pair, zero)
                x_hi = jnp.where(lane_x >= SSD_HEADDIM, x_pair, zero)
                pair_out.append(_dot(jnp.concatenate(ms, axis=1), jnp.concatenate([x_lo, x_hi], axis=0)))
            y_parts.append(jnp.concatenate(pair_out, axis=1) + y_inter)
            st_ref[d, g] = ecum[edge:edge + 1, g * hp:(g + 1) * hp] * st + _dot_tn(b_g, xd[:, g * hp:(g + 1) * hp])
        y_ref[...] = jnp.concatenate(y_parts, axis=1).astype(y_ref.dtype)


def ssd_scan(xbc, gates, pcol):
    b, L, _ = xbc.shape
    nc = L // CHUNK
    fwd = lambda b, c: (b, c, 0)
    bwd = lambda b, c: (b, nc - 1 - c, 0)
    x_blk = (None, CHUNK, SSD_CONV_CH)
    g_blk = (None, CHUNK, LANES)
    y_blk = (None, CHUNK, W_SSD)
    hp = SSD_HEADS // SSD_GROUPS * SSD_HEADDIM
    out = jax.ShapeDtypeStruct((b, L, W_SSD), BF16)
    return pl.pallas_call(
        _ssd_scan_kernel, grid=(b, nc),
        in_specs=[pl.BlockSpec(x_blk, fwd), pl.BlockSpec(g_blk, fwd),
                  pl.BlockSpec(x_blk, bwd), pl.BlockSpec(g_blk, bwd),
                  pl.BlockSpec((8, LANES), lambda b, c: (0, 0))],
        out_specs=[pl.BlockSpec(y_blk, fwd), pl.BlockSpec(y_blk, bwd)],
        out_shape=[out, out],
        scratch_shapes=[pltpu.VMEM((2, SSD_GROUPS, SSD_STATE, hp), F32)],
        compiler_params=_cparams("parallel", "arbitrary"), name="ssd_scan",
    )(xbc, gates, xbc, gates, pcol)


def _ssd_post_kernel(yf_ref, yb_ref, xs_ref, z_ref, dskip_ref, gain_ref, o_ref):
    y = yf_ref[...].astype(F32) + yb_ref[...].astype(F32) + dskip_ref[...] * xs_ref[...].astype(F32)
    o_ref[...] = _rms_rows(y * _silu(z_ref[...].astype(F32)), gain_ref[...]).astype(o_ref.dtype)


def ssd_post(yf, yb, xbc, u, dskip, gain, tl=512):
    b, L, _ = yf.shape
    blk = lambda col: pl.BlockSpec((None, tl, W_SSD), functools.partial(lambda b, i, col: (b, i, col), col=col))
    row = pl.BlockSpec((1, W_SSD), lambda b, i: (0, 0))
    return pl.pallas_call(
        _ssd_post_kernel, grid=(b, L // tl),
        in_specs=[blk(0), blk(0), blk(0), blk(2560 // W_SSD), row, row],
        out_specs=blk(0), out_shape=jax.ShapeDtypeStruct((b, L, W_SSD), BF16),
        compiler_params=_cparams("parallel", "parallel"), name="ssd_post",
    )(yf, yb, xbc, u, dskip, gain)


def _mlstm_scan_kernel(qf_ref, kf_ref, vf_ref, gf_ref, qb_ref, kb_ref, vb_ref, gb_ref, pc_ref,
                       hf_ref, hb_ref, ct_ref, m_ref):
    @pl.when(pl.program_id(1) == 0)
    def _():
        ct_ref[...] = jnp.zeros(ct_ref.shape, F32)
        m_ref[...] = jnp.zeros(m_ref.shape, F32)

    kscale = MLSTM_HD ** -0.5
    one_col = jnp.where(lax.broadcasted_iota(jnp.int32, (CHUNK, LANES), 1) == 0, 1.0, 0.0).astype(F32)
    dirs = ((qf_ref, kf_ref, vf_ref, gf_ref, hf_ref), (qb_ref, kb_ref, vb_ref, gb_ref, hb_ref))
    for d, (q_ref, k_ref, v_ref, g_ref, h_ref) in enumerate(dirs):
        keep, tri = _tri(d)
        edge = CHUNK - 1 if d == 0 else 0
        li = g_ref[:, 0:LANES] + pc_ref[0:1, :]
        fpre = g_ref[:, LANES:2 * LANES] + pc_ref[1:2, :]
        lf = jnp.minimum(fpre, 0.0) - jnp.log(1.0 + jnp.exp(-jnp.abs(fpre)))
        cum = _dot_f32(tri, lf)
        tot = cum[edge:edge + 1, :]
        cum_t = cum.T
        li_t = li.T
        m_old = m_ref[d:d + 1, :]
        inter_all = cum + m_old
        ls_all = tot - cum + li
        m_new = jnp.maximum(tot + m_old, jnp.max(ls_all, axis=0, keepdims=True))
        ws_all = jnp.exp(ls_all - m_new)
        wc_all = jnp.exp(tot + m_old - m_new)
        m_ref[d:d + 1, :] = m_new
        for h in range(MLSTM_HEADS):
            col = MLSTM_GATE_LANE + d * MLSTM_HEADS + h
            hs = slice(h * MLSTM_HD, (h + 1) * MLSTM_HD)
            q = q_ref[:, hs]
            k = k_ref[:, hs]
            v = v_ref[:, hs]
            dmat = jnp.where(keep, cum[:, col:col + 1] - cum_t[col:col + 1, :] + li_t[col:col + 1, :], -jnp.inf)
            inter = inter_all[:, col:col + 1]
            m_t = jnp.maximum(jnp.max(dmat, axis=-1, keepdims=True), inter)
            w = jnp.exp(dmat - m_t) * kscale
            w_inter = jnp.exp(inter - m_t) * kscale
            sw = _dot_nt(q, k) * w
            ct = ct_ref[d, h]
            inter_mm = _dot(q, ct.astype(BF16))
            num = _dot(sw.astype(BF16), v) + w_inter * inter_mm[:, 0:MLSTM_HD]
            den = jnp.sum(sw, axis=-1, keepdims=True) + w_inter * inter_mm[:, MLSTM_HD:MLSTM_HD + 1]
            h_ref[:, hs] = (num / jnp.maximum(jnp.abs(den), jnp.exp(-m_t))).astype(h_ref.dtype)
            ws = ws_all[:, col:col + 1]
            v_ext = jnp.concatenate([v.astype(F32) * ws, one_col * ws], axis=1).astype(BF16)
            ct_ref[d, h] = wc_all[:, col:col + 1] * ct + _dot_tn(k, v_ext)


def mlstm_scan(u, gates, pcol):
    b, L, _ = u.shape
    nc = L // CHUNK
    blk = (None, CHUNK, W_MLSTM)

    def spec(col, rev, shape=blk):
        if rev:
            return pl.BlockSpec(shape, lambda b, c: (b, nc - 1 - c, col))
        return pl.BlockSpec(shape, lambda b, c: (b, c, col))

    g_blk = (None, CHUNK, G_COLS)
    out = jax.ShapeDtypeStruct((b, L, W_MLSTM), BF16)
    qc, kc, vc = 4096 // W_MLSTM, 4608 // W_MLSTM, 5120 // W_MLSTM
    return pl.pallas_call(
        _mlstm_scan_kernel, grid=(b, nc),
        in_specs=[spec(qc, False), spec(kc, False), spec(vc, False), spec(0, False, g_blk),
                  spec(qc, True), spec(kc, True), spec(vc, True), spec(0, True, g_blk),
                  pl.BlockSpec((8, LANES), lambda b, c: (0, 0))],
        out_specs=[spec(0, False), spec(0, True)],
        out_shape=[out, out],
        scratch_shapes=[pltpu.VMEM((2, MLSTM_HEADS, MLSTM_HD, 2 * MLSTM_HD), F32), pltpu.VMEM((8, LANES), F32)],
        compiler_params=_cparams("parallel", "arbitrary"), name="mlstm_scan",
    )(u, u, u, gates, u, u, u, gates, pcol)


def _mlstm_post_kernel(hf_ref, hb_ref, o_gate_ref, gain_ref, o_ref):
    hm = _sigmoid(o_gate_ref[...].astype(F32)) * (hf_ref[...].astype(F32) + hb_ref[...].astype(F32))
    for h in range(MLSTM_HEADS):
        hs = slice(h * MLSTM_HD, (h + 1) * MLSTM_HD)
        o_ref[:, hs] = _rms_rows(hm[:, hs], gain_ref[:, hs]).astype(o_ref.dtype)


def mlstm_post(hf, hb, u, gain, tl=512):
    b, L, _ = hf.shape
    blk = lambda col: pl.BlockSpec((None, tl, W_MLSTM), functools.partial(lambda b, i, col: (b, i, col), col=col))
    return pl.pallas_call(
        _mlstm_post_kernel, grid=(b, L // tl),
        in_specs=[blk(0), blk(0), blk(5632 // W_MLSTM), pl.BlockSpec((1, W_MLSTM), lambda b, i: (0, 0))],
        out_specs=blk(0), out_shape=jax.ShapeDtypeStruct((b, L, W_MLSTM), BF16),
        compiler_params=_cparams("parallel", "parallel"), name="mlstm_post",
    )(hf, hb, u, gain)


def _xattn_kernel(q_ref, k_ref, v_ref, o_ref):
    scale = XATTN_HD ** -0.5
    for h in range(XATTN_HEADS):
        hs = slice(h * XATTN_HD, (h + 1) * XATTN_HD)
        s = _dot_nt(q_ref[:, hs], k_ref[:, hs]) * scale
        e = jnp.exp(s - jnp.max(s, axis=-1, keepdims=True))
        p = e / jnp.sum(e, axis=-1, keepdims=True)
        o_ref[:, hs] = _dot(p.astype(BF16), v_ref[:, hs]).astype(o_ref.dtype)


def xattn_core(q, kv, tq=512):
    b, L, d = q.shape
    m = kv.shape[1]
    return pl.pallas_call(
        _xattn_kernel, grid=(b, L // tq),
        in_specs=[pl.BlockSpec((None, tq, d), lambda b, i: (b, i, 0)),
                  pl.BlockSpec((None, m, d), lambda b, i: (b, 0, 0)),
                  pl.BlockSpec((None, m, d), lambda b, i: (b, 0, 1))],
        out_specs=pl.BlockSpec((None, tq, d), lambda b, i: (b, i, 0)),
        out_shape=jax.ShapeDtypeStruct((b, L, d), BF16),
        compiler_params=_cparams("parallel", "parallel"), name="xattn_core",
    )(q, kv, kv)


def _prepare_layer(l, p):
    w_in = p['w_in'][l]
    w_main = jnp.concatenate([w_in[:, :4096], w_in[:, 4112:6160]], axis=1).astype(BF16)
    w_dt = w_in[:, 4096:4112]
    w_mg = w_in[:, 6160:6176].reshape(D_MODEL, 2, 2, MLSTM_HEADS)
    w_i = w_mg[:, :, 0, :].reshape(D_MODEL, 2 * MLSTM_HEADS)
    w_f = w_mg[:, :, 1, :].reshape(D_MODEL, 2 * MLSTM_HEADS)
    zeros = lambda n: jnp.zeros((D_MODEL, n), F32)
    n_g = 2 * MLSTM_HEADS
    tile_a = jnp.concatenate([w_dt, w_i, zeros(LANES - MLSTM_GATE_LANE - n_g)], axis=1)
    tile_b = jnp.concatenate([zeros(MLSTM_GATE_LANE), w_f, zeros(LANES - MLSTM_GATE_LANE - n_g)], axis=1)
    w_gates = jnp.concatenate([tile_a, tile_b], axis=1).astype(BF16)

    def lane_row(*pieces):
        v = jnp.concatenate([jnp.ravel(x).astype(F32) for x in pieces])
        return jnp.pad(v, (0, LANES - v.shape[0]))

    gb = p['mlstm_gate_bias'][l]
    zero16 = jnp.zeros((MLSTM_GATE_LANE,), F32)
    pcol = jnp.stack([lane_row(p['ssd_dt_bias'][l], gb[:, 0, :]),
                      lane_row(zero16, gb[:, 1, :]),
                      lane_row(p['ssd_a_log'][l])] + [jnp.zeros((LANES,), F32)] * 5)

    lam_init = 0.8 - 0.6 * math.exp(-0.3 * l)
    lp = p['diff_lambda'][l].astype(F32)
    lam = jnp.exp(jnp.sum(lp[0] * lp[1])) - jnp.exp(jnp.sum(lp[2] * lp[3])) + lam_init
    slopes = jnp.exp2(-8.0 / DIFF_HEADS * jnp.arange(1, DIFF_HEADS + 1, dtype=F32))
    scalars = jnp.concatenate([slopes, lam[None], jnp.zeros((3,), F32)])

    row = lambda v: v.reshape(1, -1).astype(F32)
    return dict(
        mix_norm=row(p['mix_norm'][l]), w_main=w_main, w_gates=w_gates, pcol=pcol,
        conv_w=p['conv_dw_w'][l], conv_b=row(p['conv_dw_b'][l]),
        conv_ln_g=row(p['conv_ln_g'][l]), conv_ln_b=row(p['conv_ln_b'][l]),
        conv_pw=p['conv_pw'][l].astype(BF16),
        diff_scalars=scalars, diff_subln=row(p['diff_subln'][l]), diff_out_scale=1.0 - lam_init,
        ssd_conv_w=p['ssd_conv_w'][l], ssd_conv_b=row(p['ssd_conv_b'][l]),
        ssd_d=row(jnp.repeat(p['ssd_d'][l], SSD_HEADDIM)), ssd_norm=row(p['ssd_norm'][l]),
        mlstm_norm=row(p['mlstm_norm'][l]), w_out=p['w_out'][l].astype(BF16),
        xattn_norm=row(p['xattn_norm'][l]), mem_norm=row(p['mem_norm'][l]),
        w_xq=p['w_xq'][l].astype(BF16),
        w_xkv=jnp.concatenate([p['w_xk'][l], p['w_xv'][l]], axis=1).astype(BF16),
        w_xo=p['w_xo'][l].astype(BF16),
        ffn_norm=row(p['ffn_norm'][l]), w_gate=p['w_gate'][l].astype(BF16),
        w_up=p['w_up'][l].astype(BF16), w_down=p['w_down'][l].astype(BF16),
    )


def _mixer(x, b, L, lp):
    t = b * L
    u, gates = norm_proj(x, lp['mix_norm'], lp['w_main'], lp['w_gates'])
    u = u.reshape(b, L, U_COLS)
    gates = gates.reshape(b, L, G_COLS)
    y_conv = conv_branch(u, lp['conv_w'], lp['conv_b'], lp['conv_ln_g'], lp['conv_ln_b'], lp['conv_pw'])
    y_diff = diff_attention(u, lp['diff_scalars'], lp['diff_subln'], lp['diff_out_scale'])
    xbc = ssd_conv(u, lp['ssd_conv_w'], lp['ssd_conv_b'])
    yf, yb = ssd_scan(xbc, gates, lp['pcol'])
    y_ssd = ssd_post(yf, yb, xbc, u, lp['ssd_d'], lp['ssd_norm'])
    hf, hb = mlstm_scan(u, gates, lp['pcol'])
    y_mlstm = mlstm_post(hf, hb, u, lp['mlstm_norm'])
    acts = [a.reshape(t, -1) for a in (y_conv, y_diff, y_ssd, y_mlstm)]
    return mm_residual(x, acts, lp['w_out'])


def _trunk(x, mem, layers, final_norm):
    b, L, d = x.shape
    m = mem.shape[1]
    x = x.reshape(b * L, d)
    mem = mem.reshape(b * m, d)
    for lp in layers:
        x = _mixer(x, b, L, lp)
        q = norm_proj(x, lp['xattn_norm'], lp['w_xq'])
        kv = norm_proj(mem, lp['mem_norm'], lp['w_xkv'])
        o = xattn_core(q.reshape(b, L, d), kv.reshape(b, m, 2 * d))
        x = mm_residual(x, [o.reshape(b * L, d)], lp['w_xo'])
        hidden = norm_swiglu(x, lp['ffn_norm'], lp['w_gate'], lp['w_up'])
        x = mm_residual(x, [hidden], lp['w_down'])
    return rmsnorm(x, final_norm.reshape(1, d)).reshape(b, L, d)


def kernel(x_prompt, x_sample, mem_prompt, mem_sample, mix_norm, w_in, conv_dw_w, conv_dw_b, conv_ln_g,
           conv_ln_b, conv_pw, diff_lambda, diff_subln, ssd_conv_w, ssd_conv_b, ssd_a_log, ssd_dt_bias,
           ssd_d, ssd_norm, mlstm_gate_bias, mlstm_norm, w_out, xattn_norm, mem_norm, w_xq, w_xk, w_xv,
           w_xo, ffn_norm, w_gate, w_up, w_down, final_norm):
    p = dict(mix_norm=mix_norm, w_in=w_in, conv_dw_w=conv_dw_w, conv_dw_b=conv_dw_b,
             conv_ln_g=conv_ln_g, conv_ln_b=conv_ln_b, conv_pw=conv_pw,
             diff_lambda=diff_lambda, diff_subln=diff_subln,
             ssd_conv_w=ssd_conv_w, ssd_conv_b=ssd_conv_b, ssd_a_log=ssd_a_log,
             ssd_dt_bias=ssd_dt_bias, ssd_d=ssd_d, ssd_norm=ssd_norm,
             mlstm_gate_bias=mlstm_gate_bias, mlstm_norm=mlstm_norm, w_out=w_out,
             xattn_norm=xattn_norm, mem_norm=mem_norm, w_xq=w_xq, w_xk=w_xk, w_xv=w_xv, w_xo=w_xo,
             ffn_norm=ffn_norm, w_gate=w_gate, w_up=w_up, w_down=w_down)
    layers = [_prepare_layer(l, p) for l in range(DEPTH)]
    y_prompt = _trunk(x_prompt, mem_prompt, layers, final_norm)
    y_sample = _trunk(x_sample, mem_sample, layers, final_norm)
    return (y_prompt, y_sample)
```

```python
import functools
import math

import jax
import jax.numpy as jnp
from jax import lax
from jax.experimental import pallas as pl
from jax.experimental.pallas import tpu as pltpu

F32 = jnp.float32
BF16 = jnp.bfloat16
HIGHEST = lax.Precision.HIGHEST

D_MODEL = 2048
DEPTH = 4
W_CONV = 512
W_DIFF = 512
W_SSD = 512
W_MLSTM = 512
CONV_K = 31
DIFF_HEADS = 4
DIFF_V = 128
DIFF_QK = 64
SSD_HEADDIM = 64
SSD_HEADS = 8
SSD_GROUPS = 2
SSD_STATE = 128
SSD_CONV_K = 5
SSD_CONV_CH = W_SSD + 2 * SSD_GROUPS * SSD_STATE
CHUNK = 128
MLSTM_HEADS = 4
MLSTM_HD = 128
XATTN_HEADS = 4
XATTN_HD = D_MODEL // XATTN_HEADS
D_FF = 5632
RMS_EPS = 1e-6
LN_EPS = 1e-5

LANES = 128
HALO = 16
U_COLS = 6144
G_COLS = 256
MLSTM_GATE_LANE = 16
VMEM_LIMIT = 52 * 1024 * 1024


def _cparams(*sem):
    return pltpu.CompilerParams(dimension_semantics=sem, vmem_limit_bytes=VMEM_LIMIT)


def _sigmoid(x):
    return 1.0 / (1.0 + jnp.exp(-x))


def _silu(x):
    return x * _sigmoid(x)


def _softplus(x):
    return jnp.maximum(x, 0.0) + jnp.log(1.0 + jnp.exp(-jnp.abs(x)))


def _dot(a, b):
    return jnp.dot(a, b, preferred_element_type=F32)


def _dot_nt(a, b):
    return lax.dot_general(a, b, (((1,), (1,)), ((), ())), preferred_element_type=F32)


def _dot_tn(a, b):
    return lax.dot_general(a, b, (((0,), (0,)), ((), ())), preferred_element_type=F32)


def _dot_f32(a, b):
    return jnp.dot(a, b, preferred_element_type=F32, precision=HIGHEST)


def _rms_rows(x, gain):
    ms = jnp.mean(x * x, axis=-1, keepdims=True)
    return x * lax.rsqrt(ms + RMS_EPS) * gain


def _norm_proj_kernel(x_ref, gain_ref, w_ref, o_ref, xn_ref):
    @pl.when(pl.program_id(1) == 0)
    def _():
        xn_ref[...] = _rms_rows(x_ref[...], gain_ref[...]).astype(BF16)

    o_ref[...] = _dot(xn_ref[...], w_ref[...]).astype(o_ref.dtype)


def _norm_proj_gates_kernel(x_ref, gain_ref, w_ref, wg_ref, o_ref, og_ref, xn_ref):
    @pl.when(pl.program_id(1) == 0)
    def _():
        xn_ref[...] = _rms_rows(x_ref[...], gain_ref[...]).astype(BF16)
        og_ref[...] = _dot(xn_ref[...], wg_ref[...])

    o_ref[...] = _dot(xn_ref[...], w_ref[...]).astype(o_ref.dtype)


def _row_tile(t):
    return 1024 if t % 1024 == 0 else 512


def norm_proj(x, gain, w, wg=None):
    t, d = x.shape
    n = w.shape[1]
    tm = _row_tile(t)
    tn = 1024 if n % 1024 == 0 else 512
    grid = (t // tm, n // tn)
    x_spec = pl.BlockSpec((tm, d), lambda i, j: (i, 0))
    gain_spec = pl.BlockSpec((1, d), lambda i, j: (0, 0))
    w_spec = pl.BlockSpec((d, tn), lambda i, j: (0, j))
    o_spec = pl.BlockSpec((tm, tn), lambda i, j: (i, j))
    scratch = [pltpu.VMEM((tm, d), BF16)]
    if wg is None:
        return pl.pallas_call(
            _norm_proj_kernel, grid=grid,
            in_specs=[x_spec, gain_spec, w_spec], out_specs=o_spec,
            out_shape=jax.ShapeDtypeStruct((t, n), BF16), scratch_shapes=scratch,
            compiler_params=_cparams("parallel", "arbitrary"), name="norm_proj",
        )(x, gain, w)
    ng = wg.shape[1]
    return pl.pallas_call(
        _norm_proj_gates_kernel, grid=grid,
        in_specs=[x_spec, gain_spec, w_spec, pl.BlockSpec((d, ng), lambda i, j: (0, 0))],
        out_specs=[o_spec, pl.BlockSpec((tm, ng), lambda i, j: (i, 0))],
        out_shape=[jax.ShapeDtypeStruct((t, n), BF16), jax.ShapeDtypeStruct((t, ng), F32)],
        scratch_shapes=scratch,
        compiler_params=_cparams("parallel", "arbitrary"), name="norm_proj_gates",
    )(x, gain, w, wg)


def _norm_swiglu_kernel(x_ref, gain_ref, wg_ref, wu_ref, o_ref, xn_ref):
    @pl.when(pl.program_id(1) == 0)
    def _():
        xn_ref[...] = _rms_rows(x_ref[...], gain_ref[...]).astype(BF16)

    xn = xn_ref[...]
    o_ref[...] = (_silu(_dot(xn, wg_ref[...])) * _dot(xn, wu_ref[...])).astype(o_ref.dtype)


def norm_swiglu(x, gain, wg, wu, tn=512):
    t, d = x.shape
    n = wg.shape[1]
    tm = _row_tile(t)
    w_spec = pl.BlockSpec((d, tn), lambda i, j: (0, j))
    return pl.pallas_call(
        _norm_swiglu_kernel, grid=(t // tm, n // tn),
        in_specs=[pl.BlockSpec((tm, d), lambda i, j: (i, 0)),
                  pl.BlockSpec((1, d), lambda i, j: (0, 0)), w_spec, w_spec],
        out_specs=pl.BlockSpec((tm, tn), lambda i, j: (i, j)),
        out_shape=jax.ShapeDtypeStruct((t, n), BF16),
        scratch_shapes=[pltpu.VMEM((tm, d), BF16)],
        compiler_params=_cparams("parallel", "arbitrary"), name="norm_swiglu",
    )(x, gain, wg, wu)


def _mm_residual_kernel(*refs, n_in):
    res_ref = refs[0]
    a_refs = refs[1:1 + n_in]
    w_refs = refs[1 + n_in:1 + 2 * n_in]
    o_ref = refs[1 + 2 * n_in]
    acc = res_ref[...]
    for a_ref, w_ref in zip(a_refs, w_refs):
        acc = acc + _dot(a_ref[...], w_ref[...])
    o_ref[...] = acc


def mm_residual(res, acts, w):
    t, n = res.shape
    k = acts[0].shape[1]
    n_in = len(acts)
    tm = 1024
    tn = 1024 if k * n_in <= 2048 else 512
    a_spec = pl.BlockSpec((tm, k), lambda i, j: (i, 0))
    w_specs = [pl.BlockSpec((k, tn), functools.partial(lambda i, j, g: (g, j), g=g)) for g in range(n_in)]
    rspec = pl.BlockSpec((tm, tn), lambda i, j: (i, j))
    return pl.pallas_call(
        functools.partial(_mm_residual_kernel, n_in=n_in), grid=(t // tm, n // tn),
        in_specs=[rspec] + [a_spec] * n_in + w_specs, out_specs=rspec,
        out_shape=jax.ShapeDtypeStruct((t, n), F32),
        compiler_params=_cparams("parallel", "arbitrary"), name="mm_residual",
    )(res, *acts, *([w] * n_in))


def _rmsnorm_kernel(x_ref, gain_ref, o_ref):
    o_ref[...] = _rms_rows(x_ref[...], gain_ref[...])


def rmsnorm(x, gain):
    t, d = x.shape
    tm = 512
    return pl.pallas_call(
        _rmsnorm_kernel, grid=(t // tm,),
        in_specs=[pl.BlockSpec((tm, d), lambda i: (i, 0)), pl.BlockSpec((1, d), lambda i: (0, 0))],
        out_specs=pl.BlockSpec((tm, d), lambda i: (i, 0)),
        out_shape=jax.ShapeDtypeStruct((t, d), F32),
        compiler_params=_cparams("parallel"), name="final_rmsnorm",
    )(x, gain)


CONV_ROWS = 32


def _fill_ext(ext_ref, prev, cur, nxt, tl):
    i = pl.program_id(1)
    last = pl.num_programs(1) - 1
    ext_ref[pl.ds(0, HALO), :] = jnp.where(i > 0, prev, 0.0)
    ext_ref[pl.ds(HALO, tl), :] = cur
    ext_ref[pl.ds(HALO + tl, HALO), :] = jnp.where(i < last, nxt, 0.0)


def _dwconv_rows(ext_ref, w_ref, bias_ref, n_taps, r0):
    pad = n_taps // 2
    acc = None
    for k in range(n_taps):
        term = w_ref[pl.ds(k, 1), :] * ext_ref[pl.ds(HALO + r0 + k - pad, CONV_ROWS), :]
        acc = term if acc is None else acc + term
    return acc + bias_ref[...]


def _conv_branch_kernel(cvp_ref, cgp_ref, cv_ref, cg_ref, cvn_ref, cgn_ref,
                        w_ref, b_ref, lng_ref, lnb_ref, pw_ref, o_ref, ext_ref, act_ref, *, tl):
    def glu(v_ref, g_ref):
        return v_ref[...].astype(F32) * _sigmoid(g_ref[...].astype(F32))

    _fill_ext(ext_ref, glu(cvp_ref, cgp_ref), glu(cv_ref, cg_ref), glu(cvn_ref, cgn_ref), tl)
    for r0 in range(0, tl, CONV_ROWS):
        y = _dwconv_rows(ext_ref, w_ref, b_ref, CONV_K, r0)
        mu = jnp.mean(y, axis=-1, keepdims=True)
        yc = y - mu
        var = jnp.mean(yc * yc, axis=-1, keepdims=True)
        yn = yc * lax.rsqrt(var + LN_EPS) * lng_ref[...] + lnb_ref[...]
        act_ref[pl.ds(r0, CONV_ROWS), :] = _silu(yn).astype(BF16)
    o_ref[...] = _dot(act_ref[...], pw_ref[...]).astype(o_ref.dtype)


def _halo_specs(seq_len, tl, width, col):
    r = tl // HALO
    last_halo = seq_len // HALO - 1

    def prev_map(b, i):
        return (b, jnp.maximum(i * r - 1, 0), col)

    def cur_map(b, i):
        return (b, i, col)

    def next_map(b, i):
        return (b, jnp.minimum((i + 1) * r, last_halo), col)

    return (pl.BlockSpec((None, HALO, width), prev_map),
            pl.BlockSpec((None, tl, width), cur_map),
            pl.BlockSpec((None, HALO, width), next_map))


def conv_branch(u, w, bias, ln_g, ln_b, pw, tl=256):
    b, L, _ = u.shape
    vp, vc, vn = _halo_specs(L, tl, W_CONV, 0)
    gp, gc, gn = _halo_specs(L, tl, W_CONV, 1)
    row = pl.BlockSpec((1, W_CONV), lambda b, i: (0, 0))
    return pl.pallas_call(
        functools.partial(_conv_branch_kernel, tl=tl), grid=(b, L // tl),
        in_specs=[vp, gp, vc, gc, vn, gn,
                  pl.BlockSpec((CONV_K, W_CONV), lambda b, i: (0, 0)), row, row, row,
                  pl.BlockSpec((W_CONV, W_CONV), lambda b, i: (0, 0))],
        out_specs=pl.BlockSpec((None, tl, W_CONV), lambda b, i: (b, i, 0)),
        out_shape=jax.ShapeDtypeStruct((b, L, W_CONV), BF16),
        scratch_shapes=[pltpu.VMEM((tl + 2 * HALO, W_CONV), F32), pltpu.VMEM((tl, W_CONV), BF16)],
        compiler_params=_cparams("parallel", "arbitrary"), name="conv_branch",
    )(u, u, u, u, u, u, w, bias, ln_g, ln_b, pw)


def _ssd_conv_kernel(xp_ref, x_ref, xn_ref, w_ref, b_ref, o_ref, ext_ref, *, tl):
    _fill_ext(ext_ref, xp_ref[...].astype(F32), x_ref[...].astype(F32), xn_ref[...].astype(F32), tl)
    for r0 in range(0, tl, CONV_ROWS):
        y = _dwconv_rows(ext_ref, w_ref, b_ref, SSD_CONV_K, r0)
        o_ref[pl.ds(r0, CONV_ROWS), :] = _silu(y).astype(o_ref.dtype)


def ssd_conv(u, w, bias, tl=256):
    b, L, _ = u.shape
    xp, xc, xn = _halo_specs(L, tl, SSD_CONV_CH, 3)
    return pl.pallas_call(
        functools.partial(_ssd_conv_kernel, tl=tl), grid=(b, L // tl),
        in_specs=[xp, xc, xn, pl.BlockSpec((SSD_CONV_K, SSD_CONV_CH), lambda b, i: (0, 0)),
                  pl.BlockSpec((1, SSD_CONV_CH), lambda b, i: (0, 0))],
        out_specs=pl.BlockSpec((None, tl, SSD_CONV_CH), lambda b, i: (b, i, 0)),
        out_shape=jax.ShapeDtypeStruct((b, L, SSD_CONV_CH), BF16),
        scratch_shapes=[pltpu.VMEM((tl + 2 * HALO, SSD_CONV_CH), F32)],
        compiler_params=_cparams("parallel", "arbitrary"), name="ssd_conv",
    )(u, u, u, w, bias)


def _diff_attn_kernel(sc_ref, q_ref, k_ref, v_ref, gain_ref, o_ref,
                      q2_ref, m_ref, l_ref, acc_ref, *, blk, out_scale):
    h = pl.program_id(1)
    qi = pl.program_id(2)
    kj = pl.program_id(3)
    rows = 2 * blk
    slope = sc_ref[h]

    @pl.when(kj == 0)
    def _():
        q = q_ref[...]
        lane = lax.broadcasted_iota(jnp.int32, q.shape, 1)
        qs = (q.astype(F32) * (DIFF_QK ** -0.5)).astype(BF16)
        zero = jnp.zeros_like(qs)
        ones = jnp.where(lane < 2, 1.0, 0.0).astype(BF16)
        q2_ref[0:blk, 0:LANES] = jnp.where(lane < DIFF_QK, qs, zero)
        q2_ref[blk:rows, 0:LANES] = jnp.where(lane >= DIFF_QK, qs, zero)
        q2_ref[0:blk, LANES:2 * LANES] = ones
        q2_ref[blk:rows, LANES:2 * LANES] = ones
        m_ref[...] = jnp.full(m_ref.shape, -jnp.inf, F32)
        l_ref[...] = jnp.zeros(l_ref.shape, F32)
        acc_ref[...] = jnp.zeros(acc_ref.shape, F32)

    sign = jnp.where(kj > qi, -1.0, 1.0)
    c_idx = lax.broadcasted_iota(jnp.int32, (blk, LANES), 0)
    b_lane = lax.broadcasted_iota(jnp.int32, (blk, LANES), 1)
    col_lo = jnp.bitwise_and(c_idx, 255).astype(F32)
    col_hi = (c_idx - jnp.bitwise_and(c_idx, 255)).astype(F32)
    btile = (sign * slope) * jnp.where(b_lane == 0, col_hi, jnp.where(b_lane == 1, col_lo, 0.0))
    k_aug = jnp.concatenate([k_ref[...], btile.astype(BF16)], axis=1)
    v = v_ref[...]

    r_idx = lax.broadcasted_iota(jnp.int32, (rows, LANES), 0)
    r_idx = jnp.where(r_idx >= blk, r_idx - blk, r_idx)
    delta = (sign * slope) * ((kj - qi) * blk - r_idx).astype(F32)

    def update(t):
        m_prev = m_ref[...]
        m_next = jnp.maximum(m_prev, jnp.max(t, axis=1, keepdims=True) + delta)
        shift = delta - m_next
        p = jnp.exp(t + jnp.concatenate([shift] * (blk // LANES), axis=1))
        alpha = jnp.exp(m_prev - m_next)
        l_ref[...] = alpha * l_ref[...] + jnp.sum(p, axis=1, keepdims=True)
        acc_ref[...] = alpha * acc_ref[...] + _dot(p.astype(BF16), v)
        m_ref[...] = m_next

    @pl.when(kj != qi)
    def _():
        update(_dot_nt(q2_ref[...], k_aug))

    @pl.when(kj == qi)
    def _():
        rr = lax.broadcasted_iota(jnp.int32, (rows, blk), 0)
        rr = jnp.where(rr >= blk, rr - blk, rr)
        cc = lax.broadcasted_iota(jnp.int32, (rows, blk), 1)
        corr = (2.0 * slope) * jnp.maximum(cc - rr, 0).astype(F32)
        update(_dot_nt(q2_ref[...], k_aug) - corr)

    @pl.when(kj == pl.num_programs(3) - 1)
    def _():
        lam = sc_ref[DIFF_HEADS]
        o = acc_ref[0:blk, :] / l_ref[0:blk, :] - lam * (acc_ref[blk:rows, :] / l_ref[blk:rows, :])
        o_ref[...] = (_rms_rows(o, gain_ref[...]) * out_scale).astype(o_ref.dtype)


def diff_attention(u, scalars, subln, out_scale, blk=512):
    b, L, _ = u.shape
    assert blk % 256 == 0 and blk <= 512 and L % blk == 0
    q_blk, k_blk, v_blk = 1024 // LANES, 1536 // LANES, 2048 // LANES
    stat = lambda: pltpu.VMEM((2 * blk, LANES), F32)
    return pl.pallas_call(
        functools.partial(_diff_attn_kernel, blk=blk, out_scale=out_scale),
        grid=(b, DIFF_HEADS, L // blk, L // blk),
        in_specs=[pl.BlockSpec(memory_space=pltpu.SMEM),
                  pl.BlockSpec((None, blk, LANES), lambda b, h, i, j: (b, i, q_blk + h)),
                  pl.BlockSpec((None, blk, LANES), lambda b, h, i, j: (b, j, k_blk + h)),
                  pl.BlockSpec((None, blk, LANES), lambda b, h, i, j: (b, j, v_blk + h)),
                  pl.BlockSpec((1, DIFF_V), lambda b, h, i, j: (0, 0))],
        out_specs=pl.BlockSpec((None, blk, DIFF_V), lambda b, h, i, j: (b, i, h)),
        out_shape=jax.ShapeDtypeStruct((b, L, W_DIFF), BF16),
        scratch_shapes=[pltpu.VMEM((2 * blk, 2 * LANES), BF16), stat(), stat(), stat()],
        compiler_params=_cparams("parallel", "parallel", "parallel", "arbitrary"), name="diff_attention",
    )(scalars, u, u, u, subln)


def _tri(direction):
    r = lax.broadcasted_iota(jnp.int32, (CHUNK, CHUNK), 0)
    c = lax.broadcasted_iota(jnp.int32, (CHUNK, CHUNK), 1)
    keep = (c <= r) if direction == 0 else (c >= r)
    return keep, jnp.where(keep, 1.0, 0.0).astype(F32)


def _ssd_scan_kernel(xf_ref, gf_ref, xb_ref, gb_ref, pc_ref, yf_ref, yb_ref, st_ref):
    @pl.when(pl.program_id(1) == 0)
    def _():
        st_ref[...] = jnp.zeros(st_ref.shape, F32)

    hp = SSD_HEADS // SSD_GROUPS * SSD_HEADDIM
    lane_x = lax.broadcasted_iota(jnp.int32, (CHUNK, LANES), 1)
    for d, (x_ref, g_ref, y_ref) in enumerate(((xf_ref, gf_ref, yf_ref), (xb_ref, gb_ref, yb_ref))):
        keep, tri = _tri(d)
        edge = CHUNK - 1 if d == 0 else 0
        dt = _softplus(g_ref[...] + pc_ref[0:1, :])
        cum = _dot_f32(tri, dt * (-jnp.exp(pc_ref[2:3, :])))
        dt_t = dt.T
        cum_t = cum.T
        er = lax.broadcasted_iota(jnp.int32, (LANES, W_SSD), 0)
        ec = lax.broadcasted_iota(jnp.int32, (LANES, W_SSD), 1)
        expand = jnp.where(er == ec // SSD_HEADDIM + d * SSD_HEADS, 1.0, 0.0).astype(F32)
        ecum = _dot_f32(jnp.exp(cum), expand)
        decay = _dot_f32(jnp.exp(cum[edge:edge + 1, :] - cum) * dt, expand)

        xs = x_ref[:, 0:W_SSD]
        bmat = x_ref[:, W_SSD:W_SSD + SSD_GROUPS * SSD_STATE]
        cmat = x_ref[:, W_SSD + SSD_GROUPS * SSD_STATE:]
        xd = (xs.astype(F32) * decay).astype(BF16)
        y_parts = []
        for g in range(SSD_GROUPS):
            b_g = bmat[:, g * SSD_STATE:(g + 1) * SSD_STATE]
            c_g = cmat[:, g * SSD_STATE:(g + 1) * SSD_STATE]
            cb = _dot_nt(c_g, b_g)
            st = st_ref[d, g]
            y_inter = _dot(c_g, st.astype(BF16)) * ecum[:, g * hp:(g + 1) * hp]
            pair_out = []
            for pr in range(SSD_HEADS // SSD_GROUPS // 2):
                ms = []
                for hh in range(2):
                    col = d * SSD_HEADS + g * (SSD_HEADS // SSD_GROUPS) + pr * 2 + hh
                    seg = cum[:, col:col + 1] - cum_t[col:col + 1, :]
                    decay_qk = jnp.exp(jnp.where(keep, seg, -jnp.inf))
                    ms.append((cb * decay_qk * dt_t[col:col + 1, :]).astype(BF16))
                x_pair = xs[:, g * hp + pr * LANES:g * hp + (pr + 1) * LANES]
                zero = jnp.zeros_like(x_pair)
                x_lo = jnp.where(lane_x < SSD_HEADDIM, x_pair, zero)
                x_hi = jnp.where(lane_x >= SSD_HEADDIM, x_pair, zero)
                pair_out.append(_dot(jnp.concatenate(ms, axis=1), jnp.concatenate([x_lo, x_hi], axis=0)))
            y_parts.append(jnp.concatenate(pair_out, axis=1) + y_inter)
            st_ref[d, g] = ecum[edge:edge + 1, g * hp:(g + 1) * hp] * st + _dot_tn(b_g, xd[:, g * hp:(g + 1) * hp])
        y_ref[...] = jnp.concatenate(y_parts, axis=1).astype(y_ref.dtype)


def ssd_scan(xbc, gates, pcol):
    b, L, _ = xbc.shape
    nc = L // CHUNK
    fwd = lambda b, c: (b, c, 0)
    bwd = lambda b, c: (b, nc - 1 - c, 0)
    x_blk = (None, CHUNK, SSD_CONV_CH)
    g_blk = (None, CHUNK, LANES)
    y_blk = (None, CHUNK, W_SSD)
    hp = SSD_HEADS // SSD_GROUPS * SSD_HEADDIM
    out = jax.ShapeDtypeStruct((b, L, W_SSD), BF16)
    return pl.pallas_call(
        _ssd_scan_kernel, grid=(b, nc),
        in_specs=[pl.BlockSpec(x_blk, fwd), pl.BlockSpec(g_blk, fwd),
                  pl.BlockSpec(x_blk, bwd), pl.BlockSpec(g_blk, bwd),
                  pl.BlockSpec((8, LANES), lambda b, c: (0, 0))],
        out_specs=[pl.BlockSpec(y_blk, fwd), pl.BlockSpec(y_blk, bwd)],
        out_shape=[out, out],
        scratch_shapes=[pltpu.VMEM((2, SSD_GROUPS, SSD_STATE, hp), F32)],
        compiler_params=_cparams("parallel", "arbitrary"), name="ssd_scan",
    )(xbc, gates, xbc, gates, pcol)


def _ssd_post_kernel(yf_ref, yb_ref, xs_ref, z_ref, dskip_ref, gain_ref, o_ref):
    y = yf_ref[...].astype(F32) + yb_ref[...].astype(F32) + dskip_ref[...] * xs_ref[...].astype(F32)
    o_ref[...] = _rms_rows(y * _silu(z_ref[...].astype(F32)), gain_ref[...]).astype(o_ref.dtype)


def ssd_post(yf, yb, xbc, u, dskip, gain, tl=512):
    b, L, _ = yf.shape
    blk = lambda col: pl.BlockSpec((None, tl, W_SSD), functools.partial(lambda b, i, col: (b, i, col), col=col))
    row = pl.BlockSpec((1, W_SSD), lambda b, i: (0, 0))
    return pl.pallas_call(
        _ssd_post_kernel, grid=(b, L // tl),
        in_specs=[blk(0), blk(0), blk(0), blk(2560 // W_SSD), row, row],
        out_specs=blk(0), out_shape=jax.ShapeDtypeStruct((b, L, W_SSD), BF16),
        compiler_params=_cparams("parallel", "parallel"), name="ssd_post",
    )(yf, yb, xbc, u, dskip, gain)


def _mlstm_scan_kernel(qf_ref, kf_ref, vf_ref, gf_ref, qb_ref, kb_ref, vb_ref, gb_ref, pc_ref,
                       hf_ref, hb_ref, ct_ref, m_ref):
    @pl.when(pl.program_id(1) == 0)
    def _():
        ct_ref[...] = jnp.zeros(ct_ref.shape, F32)
        m_ref[...] = jnp.zeros(m_ref.shape, F32)

    kscale = MLSTM_HD ** -0.5
    one_col = jnp.where(lax.broadcasted_iota(jnp.int32, (CHUNK, LANES), 1) == 0, 1.0, 0.0).astype(F32)
    dirs = ((qf_ref, kf_ref, vf_ref, gf_ref, hf_ref), (qb_ref, kb_ref, vb_ref, gb_ref, hb_ref))
    for d, (q_ref, k_ref, v_ref, g_ref, h_ref) in enumerate(dirs):
        keep, tri = _tri(d)
        edge = CHUNK - 1 if d == 0 else 0
        li = g_ref[:, 0:LANES] + pc_ref[0:1, :]
        fpre = g_ref[:, LANES:2 * LANES] + pc_ref[1:2, :]
        lf = jnp.minimum(fpre, 0.0) - jnp.log(1.0 + jnp.exp(-jnp.abs(fpre)))
        cum = _dot_f32(tri, lf)
        tot = cum[edge:edge + 1, :]
        cum_t = cum.T
        li_t = li.T
        m_old = m_ref[d:d + 1, :]
        inter_all = cum + m_old
        ls_all = tot - cum + li
        m_new = jnp.maximum(tot + m_old, jnp.max(ls_all, axis=0, keepdims=True))
        ws_all = jnp.exp(ls_all - m_new)
        wc_all = jnp.exp(tot + m_old - m_new)
        m_ref[d:d + 1, :] = m_new
        for h in range(MLSTM_HEADS):
            col = MLSTM_GATE_LANE + d * MLSTM_HEADS + h
            hs = slice(h * MLSTM_HD, (h + 1) * MLSTM_HD)
            q = q_ref[:, hs]
            k = k_ref[:, hs]
            v = v_ref[:, hs]
            dmat = jnp.where(keep, cum[:, col:col + 1] - cum_t[col:col + 1, :] + li_t[col:col + 1, :], -jnp.inf)
            inter = inter_all[:, col:col + 1]
            m_t = jnp.maximum(jnp.max(dmat, axis=-1, keepdims=True), inter)
            w = jnp.exp(dmat - m_t) * kscale
            w_inter = jnp.exp(inter - m_t) * kscale
            sw = _dot_nt(q, k) * w
            ct = ct_ref[d, h]
            inter_mm = _dot(q, ct.astype(BF16))
            num = _dot(sw.astype(BF16), v) + w_inter * inter_mm[:, 0:MLSTM_HD]
            den = jnp.sum(sw, axis=-1, keepdims=True) + w_inter * inter_mm[:, MLSTM_HD:MLSTM_HD + 1]
            h_ref[:, hs] = (num / jnp.maximum(jnp.abs(den), jnp.exp(-m_t))).astype(h_ref.dtype)
            ws = ws_all[:, col:col + 1]
            v_ext = jnp.concatenate([v.astype(F32) * ws, one_col * ws], axis=1).astype(BF16)
            ct_ref[d, h] = wc_all[:, col:col + 1] * ct + _dot_tn(k, v_ext)


def mlstm_scan(u, gates, pcol):
    b, L, _ = u.shape
    nc = L // CHUNK
    blk = (None, CHUNK, W_MLSTM)

    def spec(col, rev, shape=blk):
        if rev:
            return pl.BlockSpec(shape, lambda b, c: (b, nc - 1 - c, col))
        return pl.BlockSpec(shape, lambda b, c: (b, c, col))

    g_blk = (None, CHUNK, G_COLS)
    out = jax.ShapeDtypeStruct((b, L, W_MLSTM), BF16)
    qc, kc, vc = 4096 // W_MLSTM, 4608 // W_MLSTM, 5120 // W_MLSTM
    return pl.pallas_call(
        _mlstm_scan_kernel, grid=(b, nc),
        in_specs=[spec(qc, False), spec(kc, False), spec(vc, False), spec(0, False, g_blk),
                  spec(qc, True), spec(kc, True), spec(vc, True), spec(0, True, g_blk),
                  pl.BlockSpec((8, LANES), lambda b, c: (0, 0))],
        out_specs=[spec(0, False), spec(0, True)],
        out_shape=[out, out],
        scratch_shapes=[pltpu.VMEM((2, MLSTM_HEADS, MLSTM_HD, 2 * MLSTM_HD), F32), pltpu.VMEM((8, LANES), F32)],
        compiler_params=_cparams("parallel", "arbitrary"), name="mlstm_scan",
    )(u, u, u, gates, u, u, u, gates, pcol)


def _mlstm_post_kernel(hf_ref, hb_ref, o_gate_ref, gain_ref, o_ref):
    hm = _sigmoid(o_gate_ref[...].astype(F32)) * (hf_ref[...].astype(F32) + hb_ref[...].astype(F32))
    for h in range(MLSTM_HEADS):
        hs = slice(h * MLSTM_HD, (h + 1) * MLSTM_HD)
        o_ref[:, hs] = _rms_rows(hm[:, hs], gain_ref[:, hs]).astype(o_ref.dtype)


def mlstm_post(hf, hb, u, gain, tl=512):
    b, L, _ = hf.shape
    blk = lambda col: pl.BlockSpec((None, tl, W_MLSTM), functools.partial(lambda b, i, col: (b, i, col), col=col))
    return pl.pallas_call(
        _mlstm_post_kernel, grid=(b, L // tl),
        in_specs=[blk(0), blk(0), blk(5632 // W_MLSTM), pl.BlockSpec((1, W_MLSTM), lambda b, i: (0, 0))],
        out_specs=blk(0), out_shape=jax.ShapeDtypeStruct((b, L, W_MLSTM), BF16),
        compiler_params=_cparams("parallel", "parallel"), name="mlstm_post",
    )(hf, hb, u, gain)


def _xattn_kernel(q_ref, k_ref, v_ref, o_ref):
    scale = XATTN_HD ** -0.5
    for h in range(XATTN_HEADS):
        hs = slice(h * XATTN_HD, (h + 1) * XATTN_HD)
        s = _dot_nt(q_ref[:, hs], k_ref[:, hs]) * scale
        e = jnp.exp(s - jnp.max(s, axis=-1, keepdims=True))
        p = e / jnp.sum(e, axis=-1, keepdims=True)
        o_ref[:, hs] = _dot(p.astype(BF16), v_ref[:, hs]).astype(o_ref.dtype)


def xattn_core(q, kv, tq=512):
    b, L, d = q.shape
    m = kv.shape[1]
    return pl.pallas_call(
        _xattn_kernel, grid=(b, L // tq),
        in_specs=[pl.BlockSpec((None, tq, d), lambda b, i: (b, i, 0)),
                  pl.BlockSpec((None, m, d), lambda b, i: (b, 0, 0)),
                  pl.BlockSpec((None, m, d), lambda b, i: (b, 0, 1))],
        out_specs=pl.BlockSpec((None, tq, d), lambda b, i: (b, i, 0)),
        out_shape=jax.ShapeDtypeStruct((b, L, d), BF16),
        compiler_params=_cparams("parallel", "parallel"), name="xattn_core",
    )(q, kv, kv)


def _prepare_layer(l, p):
    w_in = p['w_in'][l]
    w_main = jnp.concatenate([w_in[:, :4096], w_in[:, 4112:6160]], axis=1).astype(BF16)
    w_dt = w_in[:, 4096:4112]
    w_mg = w_in[:, 6160:6176].reshape(D_MODEL, 2, 2, MLSTM_HEADS)
    w_i = w_mg[:, :, 0, :].reshape(D_MODEL, 2 * MLSTM_HEADS)
    w_f = w_mg[:, :, 1, :].reshape(D_MODEL, 2 * MLSTM_HEADS)
    zeros = lambda n: jnp.zeros((D_MODEL, n), F32)
    n_g = 2 * MLSTM_HEADS
    tile_a = jnp.concatenate([w_dt, w_i, zeros(LANES - MLSTM_GATE_LANE - n_g)], axis=1)
    tile_b = jnp.concatenate([zeros(MLSTM_GATE_LANE), w_f, zeros(LANES - MLSTM_GATE_LANE - n_g)], axis=1)
    w_gates = jnp.concatenate([tile_a, tile_b], axis=1).astype(BF16)

    def lane_row(*pieces):
        v = jnp.concatenate([jnp.ravel(x).astype(F32) for x in pieces])
        return jnp.pad(v, (0, LANES - v.shape[0]))

    gb = p['mlstm_gate_bias'][l]
    zero16 = jnp.zeros((MLSTM_GATE_LANE,), F32)
    pcol = jnp.stack([lane_row(p['ssd_dt_bias'][l], gb[:, 0, :]),
                      lane_row(zero16, gb[:, 1, :]),
                      lane_row(p['ssd_a_log'][l])] + [jnp.zeros((LANES,), F32)] * 5)

    lam_init = 0.8 - 0.6 * math.exp(-0.3 * l)
    lp = p['diff_lambda'][l].astype(F32)
    lam = jnp.exp(jnp.sum(lp[0] * lp[1])) - jnp.exp(jnp.sum(lp[2] * lp[3])) + lam_init
    slopes = jnp.exp2(-8.0 / DIFF_HEADS * jnp.arange(1, DIFF_HEADS + 1, dtype=F32))
    scalars = jnp.concatenate([slopes, lam[None], jnp.zeros((3,), F32)])

    row = lambda v: v.reshape(1, -1).astype(F32)
    return dict(
        mix_norm=row(p['mix_norm'][l]), w_main=w_main, w_gates=w_gates, pcol=pcol,
        conv_w=p['conv_dw_w'][l], conv_b=row(p['conv_dw_b'][l]),
        conv_ln_g=row(p['conv_ln_g'][l]), conv_ln_b=row(p['conv_ln_b'][l]),
        conv_pw=p['conv_pw'][l].astype(BF16),
        diff_scalars=scalars, diff_subln=row(p['diff_subln'][l]), diff_out_scale=1.0 - lam_init,
        ssd_conv_w=p['ssd_conv_w'][l], ssd_conv_b=row(p['ssd_conv_b'][l]),
        ssd_d=row(jnp.repeat(p['ssd_d'][l], SSD_HEADDIM)), ssd_norm=row(p['ssd_norm'][l]),
        mlstm_norm=row(p['mlstm_norm'][l]), w_out=p['w_out'][l].astype(BF16),
        xattn_norm=row(p['xattn_norm'][l]), mem_norm=row(p['mem_norm'][l]),
        w_xq=p['w_xq'][l].astype(BF16),
        w_xkv=jnp.concatenate([p['w_xk'][l], p['w_xv'][l]], axis=1).astype(BF16),
        w_xo=p['w_xo'][l].astype(BF16),
        ffn_norm=row(p['ffn_norm'][l]), w_gate=p['w_gate'][l].astype(BF16),
        w_up=p['w_up'][l].astype(BF16), w_down=p['w_down'][l].astype(BF16),
    )


def _mixer(x, b, L, lp):
    t = b * L
    u, gates = norm_proj(x, lp['mix_norm'], lp['w_main'], lp['w_gates'])
    u = u.reshape(b, L, U_COLS)
    gates = gates.reshape(b, L, G_COLS)
    y_conv = conv_branch(u, lp['conv_w'], lp['conv_b'], lp['conv_ln_g'], lp['conv_ln_b'], lp['conv_pw'])
    y_diff = diff_attention(u, lp['diff_scalars'], lp['diff_subln'], lp['diff_out_scale'])
    xbc = ssd_conv(u, lp['ssd_conv_w'], lp['ssd_conv_b'])
    yf, yb = ssd_scan(xbc, gates, lp['pcol'])
    y_ssd = ssd_post(yf, yb, xbc, u, lp['ssd_d'], lp['ssd_norm'])
    hf, hb = mlstm_scan(u, gates, lp['pcol'])
    y_mlstm = mlstm_post(hf, hb, u, lp['mlstm_norm'])
    acts = [a.reshape(t, -1) for a in (y_conv, y_diff, y_ssd, y_mlstm)]
    return mm_residual(x, acts, lp['w_out'])


def _trunk(x, mem, layers, final_norm):
    b, L, d = x.shape
    m = mem.shape[1]
    x = x.reshape(b * L, d)
    mem = mem.reshape(b * m, d)
    for lp in layers:
        x = _mixer(x, b, L, lp)
        q = norm_proj(x, lp['xattn_norm'], lp['w_xq'])
        kv = norm_proj(mem, lp['mem_norm'], lp['w_xkv'])
        o = xattn_core(q.reshape(b, L, d), kv.reshape(b, m, 2 * d))
        x = mm_residual(x, [o.reshape(b * L, d)], lp['w_xo'])
        hidden = norm_swiglu(x, lp['ffn_norm'], lp['w_gate'], lp['w_up'])
        x = mm_residual(x, [hidden], lp['w_down'])
    return rmsnorm(x, final_norm.reshape(1, d)).reshape(b, L, d)


def kernel(x_prompt, x_sample, mem_prompt, mem_sample, mix_norm, w_in, conv_dw_w, conv_dw_b, conv_ln_g,
           conv_ln_b, conv_pw, diff_lambda, diff_subln, ssd_conv_w, ssd_conv_b, ssd_a_log, ssd_dt_bias,
           ssd_d, ssd_norm, mlstm_gate_bias, mlstm_norm, w_out, xattn_norm, mem_norm, w_xq, w_xk, w_xv,
           w_xo, ffn_norm, w_gate, w_up, w_down, final_norm):
    p = dict(mix_norm=mix_norm, w_in=w_in, conv_dw_w=conv_dw_w, conv_dw_b=conv_dw_b,
             conv_ln_g=conv_ln_g, conv_ln_b=conv_ln_b, conv_pw=conv_pw,
             diff_lambda=diff_lambda, diff_subln=diff_subln,
             ssd_conv_w=ssd_conv_w, ssd_conv_b=ssd_conv_b, ssd_a_log=ssd_a_log,
             ssd_dt_bias=ssd_dt_bias, ssd_d=ssd_d, ssd_norm=ssd_norm,
             mlstm_gate_bias=mlstm_gate_bias, mlstm_norm=mlstm_norm, w_out=w_out,
             xattn_norm=xattn_norm, mem_norm=mem_norm, w_xq=w_xq, w_xk=w_xk, w_xv=w_xv, w_xo=w_xo,
             ffn_norm=ffn_norm, w_gate=w_gate, w_up=w_up, w_down=w_down)
    layers = [_prepare_layer(l, p) for l in range(DEPTH)]
    y_prompt = _trunk(x_prompt, mem_prompt, layers, final_norm)
    y_sample = _trunk(x_sample, mem_sample, layers, final_norm)
    return (y_prompt, y_sample)
```
